```python
import jax, jax.numpy as jnp
from jax import lax
import numpy as np

D_MODEL = 1024
BATCH = 8
SEQ = 4096
DEPTH = 2

CHUNK = 64
N_MIXERS = 2
N_A = (DEPTH + 1) // 2
N_B = DEPTH // 2
SHORT_CONV_WIDTH = 3
CONFORMER_CONV_WIDTH = 31
D_FF = ((8 * D_MODEL // 3 + 255) // 256) * 256
RMS_EPS = 1e-6
LN_EPS = 1e-5

kernel_name = "hybrid_shortconv_conformer_conv_encoder"


def rms_norm(x, g):
    xf = x.astype(jnp.float32)
    y = xf * lax.rsqrt(jnp.mean(xf * xf, axis=-1, keepdims=True) + RMS_EPS)
    return (y * g.astype(jnp.float32)).astype(x.dtype)


def layer_norm(x, g, b):
    xf = x.astype(jnp.float32)
    mu = jnp.mean(xf, axis=-1, keepdims=True)
    var = jnp.mean(jnp.square(xf - mu), axis=-1, keepdims=True)
    y = (xf - mu) * lax.rsqrt(var + LN_EPS)
    return (y * g.astype(jnp.float32) + b.astype(jnp.float32)).astype(x.dtype)


def causal_depthwise_conv(x, w):
    k = w.shape[0]
    return lax.conv_general_dilated(
        x, w[:, None, :].astype(x.dtype), window_strides=(1,),
        padding=[(k - 1, 0)], dimension_numbers=("NWC", "WIO", "NWC"),
        feature_group_count=x.shape[-1])


def short_gated_conv(h, w_in, w_conv, w_out):
    bcv = jnp.einsum("bsd,de->bse", h, w_in)
    gate_b, gate_c, v = jnp.split(bcv, 3, axis=-1)
    y = gate_b * causal_depthwise_conv(gate_c * v, w_conv)
    return jnp.einsum("bsd,de->bse", y, w_out)


def conformer_conv_module(h, w_pw1, b_pw1, w_dw, b_dw, ln_g, ln_b, w_pw2, b_pw2):
    u = jnp.einsum("bsd,de->bse", h, w_pw1) + b_pw1
    a, g = jnp.split(u, 2, axis=-1)
    u = a * jax.nn.sigmoid(g)
    u = causal_depthwise_conv(u, w_dw) + b_dw
    u = jax.nn.silu(layer_norm(u, ln_g, ln_b))
    return jnp.einsum("bsd,de->bse", u, w_pw2) + b_pw2


def swiglu(h, w_gate, w_up, w_down):
    gu = jax.nn.silu(jnp.einsum("bsd,df->bsf", h, w_gate)) * jnp.einsum("bsd,df->bsf", h, w_up)
    return jnp.einsum("bsf,fd->bsd", gu, w_down)


def setup_inputs(seed: int = 0) -> dict:
    key = jax.random.key(seed)
    ks = jax.random.split(key, 24)
    D, F = D_MODEL, D_FF
    nrm = lambda k, shape, fan_in: jax.random.normal(k, shape, jnp.float32) * (fan_in ** -0.5)
    gain = lambda k, shape: 1.0 + 0.02 * jax.random.normal(k, shape, jnp.float32)
    small = lambda k, shape: 0.02 * jax.random.normal(k, shape, jnp.float32)
    return {
        "x": jax.random.normal(ks[0], (BATCH, SEQ, D), jnp.float32),
        "a_norm": gain(ks[1], (N_A, D)),
        "a_w_in": nrm(ks[2], (N_A, D, 3 * D), D),
        "a_conv": nrm(ks[3], (N_A, SHORT_CONV_WIDTH, D), SHORT_CONV_WIDTH),
        "a_w_out": nrm(ks[4], (N_A, D, D), D),
        "b_norm": gain(ks[5], (N_B, D)),
        "b_w_pw1": nrm(ks[6], (N_B, D, 2 * D), D),
        "b_b_pw1": small(ks[7], (N_B, 2 * D)),
        "b_conv": nrm(ks[8], (N_B, CONFORMER_CONV_WIDTH, D), CONFORMER_CONV_WIDTH),
        "b_b_conv": small(ks[9], (N_B, D)),
        "b_ln_g": gain(ks[10], (N_B, D)),
        "b_ln_b": small(ks[11], (N_B, D)),
        "b_w_pw2": nrm(ks[12], (N_B, D, D), D),
        "b_b_pw2": small(ks[13], (N_B, D)),
        "ffn_norm": gain(ks[14], (DEPTH, D)),
        "ffn_w_gate": nrm(ks[15], (DEPTH, D, F), D),
        "ffn_w_up": nrm(ks[16], (DEPTH, D, F), D),
        "ffn_w_down": nrm(ks[17], (DEPTH, F, D), F),
        "final_norm": gain(ks[18], (D,)),
    }


def reference(x, a_norm, a_w_in, a_conv, a_w_out,
              b_norm, b_w_pw1, b_b_pw1, b_conv, b_b_conv, b_ln_g, b_ln_b, b_w_pw2, b_b_pw2,
              ffn_norm, ffn_w_gate, ffn_w_up, ffn_w_down, final_norm):
    h = x
    for i in range(DEPTH):
        j = i // N_MIXERS
        if i % N_MIXERS == 0:
            h = h + short_gated_conv(rms_norm(h, a_norm[j]), a_w_in[j], a_conv[j], a_w_out[j])
        else:
            h = h + conformer_conv_module(
                rms_norm(h, b_norm[j]), b_w_pw1[j], b_b_pw1[j], b_conv[j], b_b_conv[j],
                b_ln_g[j], b_ln_b[j], b_w_pw2[j], b_b_pw2[j])
        h = h + swiglu(rms_norm(h, ffn_norm[i]), ffn_w_gate[i], ffn_w_up[i], ffn_w_down[i])
    return rms_norm(h, final_norm)
```

```python
import functools

import jax
import jax.numpy as jnp
from jax import lax
from jax.experimental import pallas as pl
from jax.experimental.pallas import tpu as pltpu

RMS_EPS = 1e-6
LN_EPS = 1e-5

SUBLANES = 8
LANES = 128
TILE_M = 512
HALO3 = SUBLANES
HALO31 = 4 * SUBLANES
CONV_TOKENS = 16
LN_ROWS = 16
FFN_CHUNKS = (1024, 1024, 768)
VMEM_LIMIT = 56 * 1024 * 1024


def _rms_norm(x, g):
    y = x * lax.rsqrt(jnp.mean(x * x, axis=-1, keepdims=True) + RMS_EPS)
    return y * g


def _dot(a, b):
    return jnp.dot(a, b, preferred_element_type=jnp.float32)


def _shortconv_kernel(tiles_per_seq, x_ref, g_ref, win_ref, wconv_ref, wout_ref,
                      o_ref, cv_ref):
    tm, d = x_ref.shape
    i = pl.program_id(0)

    @pl.when(i % tiles_per_seq == 0)
    def _():
        cv_ref[0:HALO3, :] = jnp.zeros((HALO3, d), jnp.float32)

    @pl.when(i % tiles_per_seq != 0)
    def _():
        cv_ref[0:HALO3, :] = cv_ref[tm:tm + HALO3, :]

    x = x_ref[...]
    hn = _rms_norm(x, g_ref[...]).astype(jnp.bfloat16)
    bcv = _dot(hn, win_ref[...])
    cv_ref[HALO3:HALO3 + tm, :] = bcv[:, d:2 * d] * bcv[:, 2 * d:3 * d]
    conv = (cv_ref[HALO3 - 2:HALO3 - 2 + tm, :] * wconv_ref[0:1, :]
            + cv_ref[HALO3 - 1:HALO3 - 1 + tm, :] * wconv_ref[1:2, :]
            + cv_ref[HALO3:HALO3 + tm, :] * wconv_ref[2:3, :])
    y = (bcv[:, 0:d] * conv).astype(jnp.bfloat16)
    o_ref[...] = x + _dot(y, wout_ref[...])


def _conformer_kernel(tiles_per_seq, x_ref, g_ref, w1_ref, b1_ref, wdw_ref, bdw_ref,
                      lng_ref, lnb_ref, w2_ref, b2_ref, o_ref, tok_ref, cout_ref, act_ref):
    tm, d = x_ref.shape
    kw = wdw_ref.shape[0]
    nchunk = d // LANES
    assert nchunk == SUBLANES
    i = pl.program_id(0)

    @pl.when(i % tiles_per_seq == 0)
    def _():
        tok_ref[0:HALO31 * SUBLANES, :] = jnp.zeros((HALO31 * SUBLANES, LANES), jnp.float32)

    @pl.when(i % tiles_per_seq != 0)
    def _():
        tok_ref[0:HALO31 * SUBLANES, :] = tok_ref[tm * SUBLANES:(tm + HALO31) * SUBLANES, :]

    x = x_ref[...]
    hn = _rms_norm(x, g_ref[...]).astype(jnp.bfloat16)
    u = _dot(hn, w1_ref[...]) + b1_ref[...]
    glu = u[:, 0:d] * jax.nn.sigmoid(u[:, d:2 * d])
    for r in range(tm // SUBLANES):
        for c in range(nchunk):
            tok_ref[pl.ds((HALO31 + r * SUBLANES) * SUBLANES + c, SUBLANES, stride=SUBLANES), :] = (
                glu[r * SUBLANES:(r + 1) * SUBLANES, c * LANES:(c + 1) * LANES])

    first = HALO31 - (kw - 1)

    def conv_block(b, carry):
        t0 = b * CONV_TOKENS
        acc = [None] * CONV_TOKENS
        for j in range(CONV_TOKENS + kw - 1):
            row = pl.multiple_of((t0 + first + j) * SUBLANES, SUBLANES)
            v = tok_ref[pl.ds(row, SUBLANES), :]
            for t in range(max(0, j - kw + 1), min(CONV_TOKENS, j + 1)):
                term = v * wdw_ref[j - t]
                acc[t] = term if acc[t] is None else acc[t] + term
        for t in range(CONV_TOKENS):
            row = pl.multiple_of((t0 + t) * SUBLANES, SUBLANES)
            cout_ref[pl.ds(row, SUBLANES), :] = acc[t] + bdw_ref[...]
        return carry

    lax.fori_loop(0, tm // CONV_TOKENS, conv_block, 0)

    def norm_block(r, carry):
        t0 = r * LN_ROWS
        parts = []
        for s in range(LN_ROWS // SUBLANES):
            start = (t0 + s * SUBLANES) * SUBLANES
            parts.append(jnp.concatenate(
                [cout_ref[pl.ds(start + c, SUBLANES, stride=SUBLANES), :] for c in range(nchunk)],
                axis=-1))
        y = jnp.concatenate(parts, axis=0)
        mu = jnp.mean(y, axis=-1, keepdims=True)
        cen = y - mu
        var = jnp.mean(cen * cen, axis=-1, keepdims=True)
        v = cen * lax.rsqrt(var + LN_EPS) * lng_ref[...] + lnb_ref[...]
        act_ref[pl.ds(pl.multiple_of(t0, LN_ROWS), LN_ROWS), :] = (
            v * jax.nn.sigmoid(v)).astype(jnp.bfloat16)
        return carry

    lax.fori_loop(0, tm // LN_ROWS, norm_block, 0)
    o_ref[...] = x + _dot(act_ref[...], w2_ref[...]) + b2_ref[...]


def _ffn_kernel(final_norm, x_ref, g_ref, wg_ref, wu_ref, wd_ref, gf_ref, o_ref):
    x = x_ref[...]
    hn = _rms_norm(x, g_ref[...]).astype(jnp.bfloat16)
    acc = x
    lo = 0
    for width in FFN_CHUNKS:
        gate = _dot(hn, wg_ref[:, lo:lo + width])
        up = _dot(hn, wu_ref[:, lo:lo + width])
        gu = (gate * jax.nn.sigmoid(gate) * up).astype(jnp.bfloat16)
        acc = acc + _dot(gu, wd_ref[lo:lo + width, :])
        lo += width
    if final_norm:
        acc = _rms_norm(acc, gf_ref[...])
    o_ref[...] = acc


def _row_spec(tm, d):
    return pl.BlockSpec((tm, d), lambda i: (i, 0))


def _full_spec(shape):
    return pl.BlockSpec(shape, lambda i: (0,) * len(shape))


def _call(body, h, consts, scratch):
    n, d = h.shape
    return pl.pallas_call(
        body,
        grid=(n // TILE_M,),
        in_specs=[_row_spec(TILE_M, d)] + [_full_spec(c.shape) for c in consts],
        out_specs=_row_spec(TILE_M, d),
        out_shape=jax.ShapeDtypeStruct((n, d), jnp.float32),
        scratch_shapes=scratch,
        compiler_params=pltpu.CompilerParams(
            dimension_semantics=("arbitrary",), vmem_limit_bytes=VMEM_LIMIT),
    )(h, *consts)


def kernel(x, a_norm, a_w_in, a_conv, a_w_out, b_norm, b_w_pw1, b_b_pw1, b_conv, b_b_conv,
           b_ln_g, b_ln_b, b_w_pw2, b_b_pw2, ffn_norm, ffn_w_gate, ffn_w_up, ffn_w_down,
           final_norm):
    batch, seq, d = x.shape
    assert seq % TILE_M == 0 and sum(FFN_CHUNKS) == ffn_w_gate.shape[-1]
    assert a_norm.shape[0] == 1 and b_norm.shape[0] == 1 and ffn_norm.shape[0] == 2
    tiles_per_seq = seq // TILE_M
    bf = lambda w: w.astype(jnp.bfloat16)
    row = lambda v: v.reshape(1, -1)

    h = x.reshape(batch * seq, d)
    h = _call(
        functools.partial(_shortconv_kernel, tiles_per_seq), h,
        [row(a_norm[0]), bf(a_w_in[0]), a_conv[0], bf(a_w_out[0])],
        [pltpu.VMEM((TILE_M + HALO3, d), jnp.float32)])
    h = _call(
        functools.partial(_ffn_kernel, False), h,
        [row(ffn_norm[0]), bf(ffn_w_gate[0]), bf(ffn_w_up[0]), bf(ffn_w_down[0]),
         row(final_norm)], [])
    h = _call(
        functools.partial(_conformer_kernel, tiles_per_seq), h,
        [row(b_norm[0]), bf(b_w_pw1[0]), row(b_b_pw1[0]),
         b_conv[0].reshape(-1, SUBLANES, LANES), b_b_conv[0].reshape(SUBLANES, LANES),
         row(b_ln_g[0]), row(b_ln_b[0]), bf(b_w_pw2[0]), row(b_b_pw2[0])],
        [pltpu.VMEM(((TILE_M + HALO31) * SUBLANES, LANES), jnp.float32),
         pltpu.VMEM((TILE_M * SUBLANES, LANES), jnp.float32),
         pltpu.VMEM((TILE_M, d), jnp.bfloat16)])
    h = _call(
        functools.partial(_ffn_kernel, True), h,
        [row(ffn_norm[1]), bf(ffn_w_gate[1]), bf(ffn_w_up[1]), bf(ffn_w_down[1]),
         row(final_norm)], [])
    return h.reshape(batch, seq, d)
```

```python
import functools

import jax
import jax.numpy as jnp
from jax import lax
from jax.experimental import pallas as pl
from jax.experimental.pallas import tpu as pltpu

RMS_EPS = 1e-6
LN_EPS = 1e-5

SUBLANES = 8
LANES = 128
MXU_N = 256
TILE_M = 512
HALO3 = SUBLANES
HALO31 = 4 * SUBLANES
CONV_TOKENS = 16
LN_ROWS = 16
FFN_CHUNKS = (1024, 1024, 768)
VMEM_LIMIT = 56 * 1024 * 1024


def _rms_norm(x, g):
    y = x * lax.rsqrt(jnp.mean(x * x, axis=-1, keepdims=True) + RMS_EPS)
    return y * g


def _dot(a, b):
    return jnp.dot(a, b, preferred_element_type=jnp.float32)


def _swiglu_block(h, g_ref, wg_ref, wu_ref, wd_ref, after_dot=lambda: None):
    hn = _rms_norm(h, g_ref[...]).astype(jnp.bfloat16)
    acc = h
    lo = 0
    for width in FFN_CHUNKS:
        gate = _dot(hn, wg_ref[:, lo:lo + width])
        after_dot()
        up = _dot(hn, wu_ref[:, lo:lo + width])
        after_dot()
        gu = (gate * jax.nn.sigmoid(gate) * up).astype(jnp.bfloat16)
        acc = acc + _dot(gu, wd_ref[lo:lo + width, :])
        after_dot()
        lo += width
    return acc


def _layer0_kernel(tiles_per_seq, x_ref, g1_ref, win_ref, wconv_ref, wout_ref,
                   g2_ref, wg_ref, wu_ref, wd_ref, o_ref, cv_ref):
    tm, d = x_ref.shape
    i = pl.program_id(0)

    @pl.when(i % tiles_per_seq == 0)
    def _():
        cv_ref[0:HALO3, :] = jnp.zeros((HALO3, d), jnp.float32)

    @pl.when(i % tiles_per_seq != 0)
    def _():
        cv_ref[0:HALO3, :] = cv_ref[tm:tm + HALO3, :]

    x = x_ref[...]
    hn = _rms_norm(x, g1_ref[...]).astype(jnp.bfloat16)
    bcv = _dot(hn, win_ref[...])
    cv_ref[HALO3:HALO3 + tm, :] = bcv[:, d:2 * d] * bcv[:, 2 * d:3 * d]
    conv = (cv_ref[HALO3 - 2:HALO3 - 2 + tm, :] * wconv_ref[0:1, :]
            + cv_ref[HALO3 - 1:HALO3 - 1 + tm, :] * wconv_ref[1:2, :]
            + cv_ref[HALO3:HALO3 + tm, :] * wconv_ref[2:3, :])
    y = (bcv[:, 0:d] * conv).astype(jnp.bfloat16)
    h = x + _dot(y, wout_ref[...])
    o_ref[...] = _swiglu_block(h, g2_ref, wg_ref, wu_ref, wd_ref)


def _layer1_kernel(tiles_per_seq, x_ref, g1_ref, w1_ref, b1_ref, wdw_ref, bdw_ref,
                   lng_ref, lnb_ref, w2_ref, b2_ref, g2_ref, wg_ref, wu_ref, wd_ref, gf_ref,
                   o_ref, tok_ref, cout_ref, act_ref, xres_ref):
    tm, d = x_ref.shape
    kw = wdw_ref.shape[0]
    nchunk = d // LANES
    assert nchunk == SUBLANES
    i = pl.program_id(0)

    @pl.when(i == 0)
    def _():
        act_ref[...] = jnp.zeros(act_ref.shape, act_ref.dtype)
        xres_ref[...] = jnp.zeros(xres_ref.shape, xres_ref.dtype)

    @pl.when(i % tiles_per_seq == 0)
    def _():
        tok_ref[0:HALO31 * SUBLANES, :] = jnp.zeros((HALO31 * SUBLANES, LANES), jnp.float32)

    @pl.when(i % tiles_per_seq != 0)
    def _():
        tok_ref[0:HALO31 * SUBLANES, :] = tok_ref[tm * SUBLANES:(tm + HALO31) * SUBLANES, :]

    x = x_ref[...]
    hn = _rms_norm(x, g1_ref[...]).astype(jnp.bfloat16)
    for cg in range(d // MXU_N):
        lo = cg * MXU_N
        a = _dot(hn, w1_ref[:, lo:lo + MXU_N]) + b1_ref[:, lo:lo + MXU_N]
        gate = _dot(hn, w1_ref[:, d + lo:d + lo + MXU_N]) + b1_ref[:, d + lo:d + lo + MXU_N]
        glu = a * jax.nn.sigmoid(gate)
        for r in range(tm // SUBLANES):
            for c in range(MXU_N // LANES):
                row0 = (HALO31 + r * SUBLANES) * SUBLANES + lo // LANES + c
                tok_ref[pl.ds(row0, SUBLANES, stride=SUBLANES), :] = (
                    glu[r * SUBLANES:(r + 1) * SUBLANES, c * LANES:(c + 1) * LANES])

    first = HALO31 - (kw - 1)

    def conv_block(t0):
        acc = [None] * CONV_TOKENS
        for j in range(CONV_TOKENS + kw - 1):
            row = (t0 + first + j) * SUBLANES
            v = tok_ref[row:row + SUBLANES, :]
            for t in range(max(0, j - kw + 1), min(CONV_TOKENS, j + 1)):
                term = v * wdw_ref[j - t]
                acc[t] = term if acc[t] is None else acc[t] + term
        for t in range(CONV_TOKENS):
            row = (t0 + t) * SUBLANES
            cout_ref[row:row + SUBLANES, :] = acc[t] + bdw_ref[...]

    def norm_block(t0):
        parts = []
        for s in range(LN_ROWS // SUBLANES):
            start = (t0 + s * SUBLANES) * SUBLANES
            parts.append(jnp.concatenate(
                [cout_ref[pl.ds(start + c, SUBLANES, stride=SUBLANES), :] for c in range(nchunk)],
                axis=-1))
        y = jnp.concatenate(parts, axis=0)
        mu = jnp.mean(y, axis=-1, keepdims=True)
        cen = y - mu
        var = jnp.mean(cen * cen, axis=-1, keepdims=True)
        v = cen * lax.rsqrt(var + LN_EPS) * lng_ref[...] + lnb_ref[...]
        act_ref[t0:t0 + LN_ROWS, :] = (v * jax.nn.sigmoid(v)).astype(jnp.bfloat16)

    assert CONV_TOKENS == LN_ROWS
    work = []
    for t0 in range(0, tm, CONV_TOKENS):
        work.append(functools.partial(conv_block, t0))
        if t0 > 0:
            work.append(functools.partial(norm_block, t0 - CONV_TOKENS))
    work.append(functools.partial(norm_block, tm - CONV_TOKENS))
    n_dots = 1 + 3 * len(FFN_CHUNKS)
    per_dot = -(-len(work) // n_dots)

    def emit_work(n=per_dot):
        for _ in range(min(n, len(work))):
            work.pop(0)()

    h = xres_ref[...] + _dot(act_ref[...], w2_ref[...]) + b2_ref[...]
    xres_ref[...] = x
    emit_work()
    h = _swiglu_block(h, g2_ref, wg_ref, wu_ref, wd_ref, after_dot=emit_work)
    o_ref[...] = _rms_norm(h, gf_ref[...])
    emit_work(len(work))


def _full_spec(shape):
    return pl.BlockSpec(shape, lambda i: (0,) * len(shape))


def _call(body, h, consts, scratch, grid, in_map, out_map):
    n, d = h.shape
    return pl.pallas_call(
        body,
        grid=(grid,),
        in_specs=[pl.BlockSpec((TILE_M, d), in_map)] + [_full_spec(c.shape) for c in consts],
        out_specs=pl.BlockSpec((TILE_M, d), out_map),
        out_shape=jax.ShapeDtypeStruct((n, d), jnp.float32),
        scratch_shapes=scratch,
        compiler_params=pltpu.CompilerParams(
            dimension_semantics=("arbitrary",), vmem_limit_bytes=VMEM_LIMIT),
    )(h, *consts)


def kernel(x, a_norm, a_w_in, a_conv, a_w_out, b_norm, b_w_pw1, b_b_pw1, b_conv, b_b_conv,
           b_ln_g, b_ln_b, b_w_pw2, b_b_pw2, ffn_norm, ffn_w_gate, ffn_w_up, ffn_w_down,
           final_norm):
    batch, seq, d = x.shape
    assert seq % TILE_M == 0 and sum(FFN_CHUNKS) == ffn_w_gate.shape[-1]
    assert a_norm.shape[0] == 1 and b_norm.shape[0] == 1 and ffn_norm.shape[0] == 2
    tiles_per_seq = seq // TILE_M
    n_tiles = batch * tiles_per_seq
    bf = lambda w: w.astype(jnp.bfloat16)
    row = lambda v: v.reshape(1, -1)

    h = x.reshape(batch * seq, d)
    h = _call(
        functools.partial(_layer0_kernel, tiles_per_seq), h,
        [row(a_norm[0]), bf(a_w_in[0]), a_conv[0], bf(a_w_out[0]),
         row(ffn_norm[0]), bf(ffn_w_gate[0]), bf(ffn_w_up[0]), bf(ffn_w_down[0])],
        [pltpu.VMEM((TILE_M + HALO3, d), jnp.float32)],
        n_tiles, lambda i: (i, 0), lambda i: (i, 0))
    h = _call(
        functools.partial(_layer1_kernel, tiles_per_seq), h,
        [row(b_norm[0]), bf(b_w_pw1[0]), row(b_b_pw1[0]),
         b_conv[0].reshape(-1, SUBLANES, LANES), b_b_conv[0].reshape(SUBLANES, LANES),
         row(b_ln_g[0]), row(b_ln_b[0]), bf(b_w_pw2[0]), row(b_b_pw2[0]),
         row(ffn_norm[1]), bf(ffn_w_gate[1]), bf(ffn_w_up[1]), bf(ffn_w_down[1]),
         row(final_norm)],
        [pltpu.VMEM(((TILE_M + HALO31) * SUBLANES, LANES), jnp.float32),
         pltpu.VMEM((TILE_M * SUBLANES, LANES), jnp.float32),
         pltpu.VMEM((TILE_M, d), jnp.bfloat16),
         pltpu.VMEM((TILE_M, d), jnp.float32)],
        n_tiles + 1,
        lambda i: (jnp.minimum(i, n_tiles - 1), 0),
        lambda i: (jnp.maximum(i - 1, 0), 0))
    return h.reshape(batch, seq, d)
```

```python
import functools

import jax
import jax.numpy as jnp
from jax import lax
from jax.experimental import pallas as pl
from jax.experimental.pallas import tpu as pltpu

RMS_EPS = 1e-6
LN_EPS = 1e-5

SUBLANES = 8
LANES = 128
BF16_ROWS = 16
MXU_N = 256
TILE_M = 512
HALO3 = SUBLANES
HALO31 = 4 * SUBLANES
CONV_TOKENS = 16
LN_ROWS = 16
FFN_CHUNKS = (1024, 1024, 768)
VMEM_LIMIT = 60 * 1024 * 1024


def _rms_norm(x, g):
    y = x * lax.rsqrt(jnp.mean(x * x, axis=-1, keepdims=True) + RMS_EPS)
    return y * g


def _dot(a, b):
    return jnp.dot(a, b, preferred_element_type=jnp.float32)


def _swiglu_block(h, g_ref, wg_ref, wu_ref, wd_ref, gate_up_first=False, after_dot=lambda: None,
                  after_silu=lambda gu: None):
    hn = _rms_norm(h, g_ref[...]).astype(jnp.bfloat16)
    bounds = [(sum(FFN_CHUNKS[:c]), sum(FFN_CHUNKS[:c + 1])) for c in range(len(FFN_CHUNKS))]

    def gate_up(lo, hi):
        gate = _dot(hn, wg_ref[:, lo:hi])
        after_dot()
        up = _dot(hn, wu_ref[:, lo:hi])
        after_dot()
        return gate, up

    def down(acc, gate, up, lo, hi):
        gu = (gate * jax.nn.sigmoid(gate) * up).astype(jnp.bfloat16)
        after_silu(gu)
        return acc + _dot(gu, wd_ref[lo:hi, :])

    acc = h
    if gate_up_first:
        pairs = [gate_up(lo, hi) for lo, hi in bounds]
        for (gate, up), (lo, hi) in zip(pairs, bounds):
            acc = down(acc, gate, up, lo, hi)
    else:
        for lo, hi in bounds:
            gate, up = gate_up(lo, hi)
            acc = down(acc, gate, up, lo, hi)
    return acc


def _zero_after(v):
    bits = pltpu.bitcast(v, jnp.uint32)
    bits = jnp.right_shift(jnp.right_shift(bits, jnp.uint32(16)), jnp.uint32(16))
    return pltpu.bitcast(bits, jnp.float32)


def _layer0_kernel(tiles_per_seq, x_ref, g1_ref, win_ref, wconv_ref, wout_ref,
                   g2_ref, wg_ref, wu_ref, wd_ref, o_ref, cv_ref):
    tm, d = x_ref.shape
    i = pl.program_id(0)

    @pl.when(i % tiles_per_seq == 0)
    def _():
        cv_ref[0:HALO3, :] = jnp.zeros((HALO3, d), jnp.float32)

    @pl.when(i % tiles_per_seq != 0)
    def _():
        cv_ref[0:HALO3, :] = cv_ref[tm:tm + HALO3, :]

    x = x_ref[...]
    hn = _rms_norm(x, g1_ref[...]).astype(jnp.bfloat16)
    bcv = _dot(hn, win_ref[...])
    cv_ref[HALO3:HALO3 + tm, :] = bcv[:, d:2 * d] * bcv[:, 2 * d:3 * d]
    conv = (cv_ref[HALO3 - 2:HALO3 - 2 + tm, :] * wconv_ref[0:1, :]
            + cv_ref[HALO3 - 1:HALO3 - 1 + tm, :] * wconv_ref[1:2, :]
            + cv_ref[HALO3:HALO3 + tm, :] * wconv_ref[2:3, :])
    y = (bcv[:, 0:d] * conv).astype(jnp.bfloat16)
    h = x + _dot(y, wout_ref[...])
    o_ref[...] = _swiglu_block(h, g2_ref, wg_ref, wu_ref, wd_ref)


def _layer1_kernel(tiles_per_seq, x_ref, g1_ref, w1_ref, b1_ref, wdw_ref, bdw_ref,
                   lng_ref, lnb_ref, w2_ref, b2_ref, g2_ref, wg_ref, wu_ref, wd_ref, gf_ref,
                   o_ref, tok_ref, cout_ref, act_ref, xres_ref):
    tm, d = x_ref.shape
    kw = wdw_ref.shape[0]
    nchunk = d // LANES
    assert nchunk == SUBLANES
    i = pl.program_id(0)

    @pl.when(i == 0)
    def _():
        act_ref[...] = jnp.zeros(act_ref.shape, act_ref.dtype)
        xres_ref[...] = jnp.zeros(xres_ref.shape, xres_ref.dtype)

    @pl.when(i % tiles_per_seq == 0)
    def _():
        tok_ref[0:HALO31 * SUBLANES, :] = jnp.zeros((HALO31 * SUBLANES, LANES), jnp.float32)

    @pl.when(i % tiles_per_seq != 0)
    def _():
        tok_ref[0:HALO31 * SUBLANES, :] = tok_ref[tm * SUBLANES:(tm + HALO31) * SUBLANES, :]

    x = x_ref[...]
    hn = _rms_norm(x, g1_ref[...]).astype(jnp.bfloat16)
    for cg in range(d // MXU_N):
        lo = cg * MXU_N
        a = _dot(hn, w1_ref[:, lo:lo + MXU_N]) + b1_ref[:, lo:lo + MXU_N]
        gate = _dot(hn, w1_ref[:, d + lo:d + lo + MXU_N]) + b1_ref[:, d + lo:d + lo + MXU_N]
        glu = a * jax.nn.sigmoid(gate)
        for r in range(tm // SUBLANES):
            for c in range(MXU_N // LANES):
                row0 = (HALO31 + r * SUBLANES) * SUBLANES + lo // LANES + c
                tok_ref[pl.ds(row0, SUBLANES, stride=SUBLANES), :] = (
                    glu[r * SUBLANES:(r + 1) * SUBLANES, c * LANES:(c + 1) * LANES])

    first = HALO31 - (kw - 1)

    def conv_block(t0):
        acc = [None] * CONV_TOKENS
        for j in range(CONV_TOKENS + kw - 1):
            row = (t0 + first + j) * SUBLANES
            v = tok_ref[row:row + SUBLANES, :]
            for t in range(max(0, j - kw + 1), min(CONV_TOKENS, j + 1)):
                term = v * wdw_ref[j - t]
                acc[t] = term if acc[t] is None else acc[t] + term
        for t in range(CONV_TOKENS):
            row = (t0 + t) * SUBLANES
            cout_ref[row:row + SUBLANES, :] = acc[t] + bdw_ref[...]

    def norm_block(t0, late_zero):
        parts = []
        for s in range(LN_ROWS // SUBLANES):
            start = (t0 + s * SUBLANES) * SUBLANES
            parts.append(jnp.concatenate(
                [cout_ref[pl.ds(start + c, SUBLANES, stride=SUBLANES), :] + late_zero
                 for c in range(nchunk)], axis=-1))
        y = jnp.concatenate(parts, axis=0)
        mu = jnp.mean(y, axis=-1, keepdims=True)
        cen = y - mu
        var = jnp.mean(cen * cen, axis=-1, keepdims=True)
        v = cen * lax.rsqrt(var + LN_EPS) * lng_ref[...] + lnb_ref[...]
        act_ref[t0:t0 + LN_ROWS, :] = (v * jax.nn.sigmoid(v)).astype(jnp.bfloat16)

    work = [functools.partial(conv_block, t0) for t0 in range(0, tm, CONV_TOKENS)]
    per_dot = -(-len(work) // (1 + 2 * len(FFN_CHUNKS)))

    def emit_convs(n=per_dot):
        for _ in range(min(n, len(work))):
            work.pop(0)()

    norm_blocks = list(range(0, tm, LN_ROWS))
    per_chunk = -(-len(norm_blocks) // len(FFN_CHUNKS))

    def emit_norms(gu):
        emit_convs(len(work))
        late_zero = _zero_after(gu[0:BF16_ROWS, gu.shape[1] - LANES:])
        for _ in range(min(per_chunk, len(norm_blocks))):
            norm_block(norm_blocks.pop(0), late_zero)

    h = xres_ref[...] + _dot(act_ref[...], w2_ref[...]) + b2_ref[...]
    xres_ref[...] = x
    emit_convs()
    h = _swiglu_block(h, g2_ref, wg_ref, wu_ref, wd_ref, gate_up_first=True,
                      after_dot=emit_convs, after_silu=emit_norms)
    o_ref[...] = _rms_norm(h, gf_ref[...])


def _full_spec(shape):
    return pl.BlockSpec(shape, lambda i: (0,) * len(shape))


def _call(body, h, consts, scratch, grid, in_map, out_map):
    n, d = h.shape
    return pl.pallas_call(
        body,
        grid=(grid,),
        in_specs=[pl.BlockSpec((TILE_M, d), in_map)] + [_full_spec(c.shape) for c in consts],
        out_specs=pl.BlockSpec((TILE_M, d), out_map),
        out_shape=jax.ShapeDtypeStruct((n, d), jnp.float32),
        scratch_shapes=scratch,
        compiler_params=pltpu.CompilerParams(
            dimension_semantics=("arbitrary",), vmem_limit_bytes=VMEM_LIMIT),
    )(h, *consts)


def kernel(x, a_norm, a_w_in, a_conv, a_w_out, b_norm, b_w_pw1, b_b_pw1, b_conv, b_b_conv,
           b_ln_g, b_ln_b, b_w_pw2, b_b_pw2, ffn_norm, ffn_w_gate, ffn_w_up, ffn_w_down,
           final_norm):
    batch, seq, d = x.shape
    assert seq % TILE_M == 0 and sum(FFN_CHUNKS) == ffn_w_gate.shape[-1]
    assert a_norm.shape[0] == 1 and b_norm.shape[0] == 1 and ffn_norm.shape[0] == 2
    tiles_per_seq = seq // TILE_M
    n_tiles = batch * tiles_per_seq
    bf = lambda w: w.astype(jnp.bfloat16)
    row = lambda v: v.reshape(1, -1)

    h = x.reshape(batch * seq, d)
    h = _call(
        functools.partial(_layer0_kernel, tiles_per_seq), h,
        [row(a_norm[0]), bf(a_w_in[0]), a_conv[0], bf(a_w_out[0]),
         row(ffn_norm[0]), bf(ffn_w_gate[0]), bf(ffn_w_up[0]), bf(ffn_w_down[0])],
        [pltpu.VMEM((TILE_M + HALO3, d), jnp.float32)],
        n_tiles, lambda i: (i, 0), lambda i: (i, 0))
    h = _call(
        functools.partial(_layer1_kernel, tiles_per_seq), h,
        [row(b_norm[0]), bf(b_w_pw1[0]), row(b_b_pw1[0]),
         b_conv[0].reshape(-1, SUBLANES, LANES), b_b_conv[0].reshape(SUBLANES, LANES),
         row(b_ln_g[0]), row(b_ln_b[0]), bf(b_w_pw2[0]), row(b_b_pw2[0]),
         row(ffn_norm[1]), bf(ffn_w_gate[1]), bf(ffn_w_up[1]), bf(ffn_w_down[1]),
         row(final_norm)],
        [pltpu.VMEM(((TILE_M + HALO31) * SUBLANES, LANES), jnp.float32),
         pltpu.VMEM((TILE_M * SUBLANES, LANES), jnp.float32),
         pltpu.VMEM((TILE_M, d), jnp.bfloat16),
         pltpu.VMEM((TILE_M, d), jnp.float32)],
        n_tiles + 1,
        lambda i: (jnp.minimum(i, n_tiles - 1), 0),
        lambda i: (jnp.maximum(i - 1, 0), 0))
    return h.reshape(batch, seq, d)
```

```python
import functools

import jax
import jax.numpy as jnp
from jax import lax
from jax.experimental import pallas as pl
from jax.experimental.pallas import tpu as pltpu

RMS_EPS = 1e-6
LN_EPS = 1e-5

SUBLANES = 8
LANES = 128
BF16_ROWS = 16
MXU_N = 256
TILE_M = 512
HALO3 = SUBLANES
HALO31 = 4 * SUBLANES
CONV_TOKENS = 128
LN_ROWS = 32
FFN_CHUNKS_L0 = (768, 768, 768, 512)
FFN_CHUNKS_L1 = (512, 512, 512, 512, 512, 256)
VMEM_LIMIT = 60 * 1024 * 1024


def _rms_norm(x, g):
    y = x * lax.rsqrt(jnp.mean(x * x, axis=-1, keepdims=True) + RMS_EPS)
    return y * g


def _dot(a, b):
    return jnp.dot(a, b, preferred_element_type=jnp.float32)


def _swiglu_block(h, g_ref, wg_ref, wu_ref, wd_ref, chunks, gate_up_first=False,
                  after_dot=lambda: None, after_silu=lambda gu: None):
    hn = _rms_norm(h, g_ref[...]).astype(jnp.bfloat16)
    assert sum(chunks) == wg_ref.shape[1] and all(c % MXU_N == 0 for c in chunks)
    bounds = [(sum(chunks[:c]), sum(chunks[:c + 1])) for c in range(len(chunks))]

    def gate_up(lo, hi):
        gate = _dot(hn, wg_ref[:, lo:hi])
        after_dot()
        up = _dot(hn, wu_ref[:, lo:hi])
        after_dot()
        return gate, up

    def down(acc, gate, up, lo, hi):
        gu = (gate * jax.nn.sigmoid(gate) * up).astype(jnp.bfloat16)
        after_silu(gu)
        return acc + _dot(gu, wd_ref[lo:hi, :])

    acc = h
    if gate_up_first:
        pairs = [gate_up(lo, hi) for lo, hi in bounds]
        for (gate, up), (lo, hi) in zip(pairs, bounds):
            acc = down(acc, gate, up, lo, hi)
    else:
        for lo, hi in bounds:
            gate, up = gate_up(lo, hi)
            acc = down(acc, gate, up, lo, hi)
    return acc


def _zero_after(v):
    bits = pltpu.bitcast(v, jnp.uint32)
    bits = jnp.right_shift(jnp.right_shift(bits, jnp.uint32(16)), jnp.uint32(16))
    return pltpu.bitcast(bits, jnp.float32)


def _layer0_kernel(tiles_per_seq, x_ref, g1_ref, win_ref, wconv_ref, wout_ref,
                   g2_ref, wg_ref, wu_ref, wd_ref, o_ref, cv_ref):
    tm, d = x_ref.shape
    i = pl.program_id(0)

    @pl.when(i % tiles_per_seq == 0)
    def _():
        cv_ref[0:HALO3, :] = jnp.zeros((HALO3, d), jnp.float32)

    @pl.when(i % tiles_per_seq != 0)
    def _():
        cv_ref[0:HALO3, :] = cv_ref[tm:tm + HALO3, :]

    x = x_ref[...]
    hn = _rms_norm(x, g1_ref[...]).astype(jnp.bfloat16)
    bcv = _dot(hn, win_ref[...])
    cv_ref[HALO3:HALO3 + tm, :] = bcv[:, d:2 * d] * bcv[:, 2 * d:3 * d]
    conv = (cv_ref[HALO3 - 2:HALO3 - 2 + tm, :] * wconv_ref[0:1, :]
            + cv_ref[HALO3 - 1:HALO3 - 1 + tm, :] * wconv_ref[1:2, :]
            + cv_ref[HALO3:HALO3 + tm, :] * wconv_ref[2:3, :])
    y = (bcv[:, 0:d] * conv).astype(jnp.bfloat16)
    h = x + _dot(y, wout_ref[...])
    o_ref[...] = _swiglu_block(h, g2_ref, wg_ref, wu_ref, wd_ref, FFN_CHUNKS_L0, gate_up_first=True)


def _layer1_kernel(tiles_per_seq, x_ref, g1_ref, w1_ref, b1_ref, wdw_ref, bdw_ref,
                   lng_ref, lnb_ref, w2_ref, b2_ref, g2_ref, wg_ref, wu_ref, wd_ref, gf_ref,
                   o_ref, tok_ref, cout_ref, act_ref, xres_ref):
    tm, d = x_ref.shape
    kw = wdw_ref.shape[0]
    nchunk = d // LANES
    assert nchunk == SUBLANES
    i = pl.program_id(0)

    @pl.when(i == 0)
    def _():
        act_ref[...] = jnp.zeros(act_ref.shape, act_ref.dtype)
        xres_ref[...] = jnp.zeros(xres_ref.shape, xres_ref.dtype)

    @pl.when(i % tiles_per_seq == 0)
    def _():
        tok_ref[0:HALO31 * SUBLANES, :] = jnp.zeros((HALO31 * SUBLANES, LANES), jnp.float32)

    @pl.when(i % tiles_per_seq != 0)
    def _():
        tok_ref[0:HALO31 * SUBLANES, :] = tok_ref[tm * SUBLANES:(tm + HALO31) * SUBLANES, :]

    h = xres_ref[...] + _dot(act_ref[...], w2_ref[...]) + b2_ref[...]

    x = x_ref[...]
    xres_ref[...] = x
    hn = _rms_norm(x, g1_ref[...]).astype(jnp.bfloat16)
    for cg in range(d // MXU_N):
        lo = cg * MXU_N
        a = _dot(hn, w1_ref[:, lo:lo + MXU_N]) + b1_ref[:, lo:lo + MXU_N]
        gate = _dot(hn, w1_ref[:, d + lo:d + lo + MXU_N]) + b1_ref[:, d + lo:d + lo + MXU_N]
        glu = a * jax.nn.sigmoid(gate)
        for r in range(tm // SUBLANES):
            for c in range(MXU_N // LANES):
                row0 = (HALO31 + r * SUBLANES) * SUBLANES + lo // LANES + c
                tok_ref[pl.ds(row0, SUBLANES, stride=SUBLANES), :] = (
                    glu[r * SUBLANES:(r + 1) * SUBLANES, c * LANES:(c + 1) * LANES])

    first = HALO31 - (kw - 1)

    def conv_block(t0):
        acc = [None] * CONV_TOKENS
        for j in range(CONV_TOKENS + kw - 1):
            row = (t0 + first + j) * SUBLANES
            v = tok_ref[row:row + SUBLANES, :]
            for t in range(max(0, j - kw + 1), min(CONV_TOKENS, j + 1)):
                term = v * wdw_ref[j - t]
                acc[t] = term if acc[t] is None else acc[t] + term
        for t in range(CONV_TOKENS):
            row = (t0 + t) * SUBLANES
            cout_ref[row:row + SUBLANES, :] = acc[t] + bdw_ref[...]

    def norm_block(t0, late_zero):
        parts = []
        for s in range(LN_ROWS // SUBLANES):
            start = (t0 + s * SUBLANES) * SUBLANES
            parts.append(jnp.concatenate(
                [cout_ref[pl.ds(start + c, SUBLANES, stride=SUBLANES), :] + late_zero
                 for c in range(nchunk)], axis=-1))
        y = jnp.concatenate(parts, axis=0)
        mu = jnp.mean(y, axis=-1, keepdims=True)
        cen = y - mu
        var = jnp.mean(cen * cen, axis=-1, keepdims=True)
        v = cen * lax.rsqrt(var + LN_EPS) * lng_ref[...] + lnb_ref[...]
        act_ref[t0:t0 + LN_ROWS, :] = (v * jax.nn.sigmoid(v)).astype(jnp.bfloat16)

    work = [functools.partial(conv_block, t0) for t0 in range(0, tm, CONV_TOKENS)]
    per_dot = -(-len(work) // (1 + 2 * len(FFN_CHUNKS_L1)))

    def emit_convs(n=per_dot):
        for _ in range(min(n, len(work))):
            work.pop(0)()

    norm_blocks = list(range(0, tm, LN_ROWS))
    per_chunk = -(-len(norm_blocks) // len(FFN_CHUNKS_L1))

    def emit_norms(gu):
        emit_convs(len(work))
        late_zero = _zero_after(gu[0:BF16_ROWS, gu.shape[1] - LANES:])
        for _ in range(min(per_chunk, len(norm_blocks))):
            norm_block(norm_blocks.pop(0), late_zero)

    emit_convs()
    h = _swiglu_block(h, g2_ref, wg_ref, wu_ref, wd_ref, FFN_CHUNKS_L1, gate_up_first=True,
                      after_dot=emit_convs, after_silu=emit_norms)
    o_ref[...] = _rms_norm(h, gf_ref[...])


def _full_spec(shape):
    return pl.BlockSpec(shape, lambda i: (0,) * len(shape))


def _call(body, h, consts, scratch, grid, in_map, out_map):
    n, d = h.shape
    return pl.pallas_call(
        body,
        grid=(grid,),
        in_specs=[pl.BlockSpec((TILE_M, d), in_map)] + [_full_spec(c.shape) for c in consts],
        out_specs=pl.BlockSpec((TILE_M, d), out_map),
        out_shape=jax.ShapeDtypeStruct((n, d), jnp.float32),
        scratch_shapes=scratch,
        compiler_params=pltpu.CompilerParams(
            dimension_semantics=("arbitrary",), vmem_limit_bytes=VMEM_LIMIT),
    )(h, *consts)


def kernel(x, a_norm, a_w_in, a_conv, a_w_out, b_norm, b_w_pw1, b_b_pw1, b_conv, b_b_conv,
           b_ln_g, b_ln_b, b_w_pw2, b_b_pw2, ffn_norm, ffn_w_gate, ffn_w_up, ffn_w_down,
           final_norm):
    batch, seq, d = x.shape
    assert seq % TILE_M == 0
    assert a_norm.shape[0] == 1 and b_norm.shape[0] == 1 and ffn_norm.shape[0] == 2
    tiles_per_seq = seq // TILE_M
    n_tiles = batch * tiles_per_seq
    bf = lambda w: w.astype(jnp.bfloat16)
    row = lambda v: v.reshape(1, -1)

    h = x.reshape(batch * seq, d)
    h = _call(
        functools.partial(_layer0_kernel, tiles_per_seq), h,
        [row(a_norm[0]), bf(a_w_in[0]), a_conv[0], bf(a_w_out[0]),
         row(ffn_norm[0]), bf(ffn_w_gate[0]), bf(ffn_w_up[0]), bf(ffn_w_down[0])],
        [pltpu.VMEM((TILE_M + HALO3, d), jnp.float32)],
        n_tiles, lambda i: (i, 0), lambda i: (i, 0))
    h = _call(
        functools.partial(_layer1_kernel, tiles_per_seq), h,
        [row(b_norm[0]), bf(b_w_pw1[0]), row(b_b_pw1[0]),
         b_conv[0].reshape(-1, SUBLANES, LANES), b_b_conv[0].reshape(SUBLANES, LANES),
         row(b_ln_g[0]), row(b_ln_b[0]), bf(b_w_pw2[0]), row(b_b_pw2[0]),
         row(ffn_norm[1]), bf(ffn_w_gate[1]), bf(ffn_w_up[1]), bf(ffn_w_down[1]),
         row(final_norm)],
        [pltpu.VMEM(((TILE_M + HALO31) * SUBLANES, LANES), jnp.float32),
         pltpu.VMEM((TILE_M * SUBLANES, LANES), jnp.float32),
         pltpu.VMEM((TILE_M, d), jnp.bfloat16),
         pltpu.VMEM((TILE_M, d), jnp.float32)],
        n_tiles + 1,
        lambda i: (jnp.minimum(i, n_tiles - 1), 0),
        lambda i: (jnp.maximum(i - 1, 0), 0))
    return h.reshape(batch, seq, d)
```

```python
import functools

import jax
import jax.numpy as jnp
from jax import lax
from jax.experimental import pallas as pl
from jax.experimental.pallas import tpu as pltpu

RMS_EPS = 1e-6
LN_EPS = 1e-5

SUBLANES = 8
LANES = 128
BF16_ROWS = 16
MXU_N = 256
STAGE_COLS = 512
STAGE_ROWS = 256
PW1_COLS = 256
TILE_M = 512
HALO3 = SUBLANES
HALO31 = 4 * SUBLANES
CONV_TOKENS = 128
LN_ROWS = 32
FFN_CHUNKS_L0 = (768, 768, 768, 512)
FFN_CHUNKS_L1 = (512, 512, 512, 512, 512, 256)
VMEM_LIMIT = 60 * 1024 * 1024


def _rms_norm(x, g):
    y = x * lax.rsqrt(jnp.mean(x * x, axis=-1, keepdims=True) + RMS_EPS)
    return y * g


def _dot(a, b):
    return jnp.dot(a, b, preferred_element_type=jnp.float32)


def _swiglu_block(h, g_ref, wg_ref, wu_ref, wd_ref, chunks, gate_up_first=False,
                  after_dot=lambda: None, after_silu=lambda gu: None):
    hn = _rms_norm(h, g_ref[...]).astype(jnp.bfloat16)
    assert sum(chunks) == wg_ref.shape[1] and all(c % MXU_N == 0 for c in chunks)
    bounds = [(sum(chunks[:c]), sum(chunks[:c + 1])) for c in range(len(chunks))]

    def gate_up(lo, hi):
        gate = _dot(hn, wg_ref[:, lo:hi])
        after_dot()
        up = _dot(hn, wu_ref[:, lo:hi])
        after_dot()
        return gate, up

    def down(acc, gate, up, lo, hi):
        gu = (gate * jax.nn.sigmoid(gate) * up).astype(jnp.bfloat16)
        after_silu(gu)
        return acc + _dot(gu, wd_ref[lo:hi, :])

    acc = h
    if gate_up_first:
        pairs = [gate_up(lo, hi) for lo, hi in bounds]
        for (gate, up), (lo, hi) in zip(pairs, bounds):
            acc = down(acc, gate, up, lo, hi)
    else:
        for lo, hi in bounds:
            gate, up = gate_up(lo, hi)
            acc = down(acc, gate, up, lo, hi)
    return acc


def _zero_after(v):
    bits = pltpu.bitcast(v, jnp.uint32)
    bits = jnp.right_shift(jnp.right_shift(bits, jnp.uint32(16)), jnp.uint32(16))
    return pltpu.bitcast(bits, jnp.float32)


def _load_weights_bf16(pairs, stage_c, stage_r, sem):
    d = stage_c.shape[1]
    chunks = []
    for src, dst in pairs:
        rows, cols = dst.shape
        if rows == d:
            for lo in range(0, cols, STAGE_COLS):
                w = min(STAGE_COLS, cols - lo)
                chunks.append((src.at[:, lo:lo + w], dst.at[:, lo:lo + w], stage_c,
                               (slice(None), slice(0, w))))
        else:
            assert cols == d and rows % STAGE_ROWS == 0
            for lo in range(0, rows, STAGE_ROWS):
                chunks.append((src.at[lo:lo + STAGE_ROWS, :], dst.at[lo:lo + STAGE_ROWS, :],
                               stage_r, (slice(None), slice(None))))
    copies = [pltpu.make_async_copy(src, stage.at[(k % 2,) + idx], sem.at[k % 2])
              for k, (src, _, stage, idx) in enumerate(chunks)]
    copies[0].start()
    for k, (_, dst, stage, idx) in enumerate(chunks):
        if k + 1 < len(chunks):
            copies[k + 1].start()
        copies[k].wait()
        dst[...] = stage[(k % 2,) + idx].astype(jnp.bfloat16)


def _layer0_kernel(tiles_per_seq, x_ref, g1_ref, wconv_ref, g2_ref,
                   win_hbm, wout_hbm, wg_hbm, wu_hbm, wd_hbm, o_ref, cv_ref,
                   win_ref, wout_ref, wg_ref, wu_ref, wd_ref, stage_c, stage_r, sem):
    tm, d = x_ref.shape
    i = pl.program_id(0)

    @pl.when(i == 0)
    def _():
        _load_weights_bf16(
            [(win_hbm.at[0], win_ref), (wout_hbm.at[0], wout_ref), (wg_hbm.at[0], wg_ref),
             (wu_hbm.at[0], wu_ref), (wd_hbm.at[0], wd_ref)], stage_c, stage_r, sem)

    @pl.when(i % tiles_per_seq == 0)
    def _():
        cv_ref[0:HALO3, :] = jnp.zeros((HALO3, d), jnp.float32)

    @pl.when(i % tiles_per_seq != 0)
    def _():
        cv_ref[0:HALO3, :] = cv_ref[tm:tm + HALO3, :]

    x = x_ref[...]
    hn = _rms_norm(x, g1_ref[...]).astype(jnp.bfloat16)
    bcv = _dot(hn, win_ref[...])
    cv_ref[HALO3:HALO3 + tm, :] = bcv[:, d:2 * d] * bcv[:, 2 * d:3 * d]
    conv = (cv_ref[HALO3 - 2:HALO3 - 2 + tm, :] * wconv_ref[0:1, :]
            + cv_ref[HALO3 - 1:HALO3 - 1 + tm, :] * wconv_ref[1:2, :]
            + cv_ref[HALO3:HALO3 + tm, :] * wconv_ref[2:3, :])
    y = (bcv[:, 0:d] * conv).astype(jnp.bfloat16)
    h = x + _dot(y, wout_ref[...])
    o_ref[...] = _swiglu_block(h, g2_ref, wg_ref, wu_ref, wd_ref, FFN_CHUNKS_L0, gate_up_first=True)


def _layer1_kernel(tiles_per_seq, x_ref, g1_ref, b1_ref, wdw_ref, bdw_ref, lng_ref, lnb_ref,
                   b2_ref, g2_ref, gf_ref, w1_hbm, w2_hbm, wg_hbm, wu_hbm, wd_hbm,
                   o_ref, tok_ref, cout_ref, act_ref, xres_ref,
                   w1_ref, w2_ref, wg_ref, wu_ref, wd_ref, stage_c, stage_r, sem):
    tm, d = x_ref.shape
    kw = wdw_ref.shape[0]
    nchunk = d // LANES
    assert nchunk == SUBLANES
    i = pl.program_id(0)

    @pl.when(i == 0)
    def _():
        act_ref[...] = jnp.zeros(act_ref.shape, act_ref.dtype)
        xres_ref[...] = jnp.zeros(xres_ref.shape, xres_ref.dtype)
        _load_weights_bf16(
            [(w1_hbm.at[0], w1_ref), (w2_hbm.at[0], w2_ref), (wg_hbm.at[1], wg_ref),
             (wu_hbm.at[1], wu_ref), (wd_hbm.at[1], wd_ref)], stage_c, stage_r, sem)

    @pl.when(i % tiles_per_seq == 0)
    def _():
        tok_ref[0:HALO31 * SUBLANES, :] = jnp.zeros((HALO31 * SUBLANES, LANES), jnp.float32)

    @pl.when(i % tiles_per_seq != 0)
    def _():
        tok_ref[0:HALO31 * SUBLANES, :] = tok_ref[tm * SUBLANES:(tm + HALO31) * SUBLANES, :]

    h = xres_ref[...] + _dot(act_ref[...], w2_ref[...]) + b2_ref[...]

    x = x_ref[...]
    xres_ref[...] = x
    hn = _rms_norm(x, g1_ref[...]).astype(jnp.bfloat16)
    for lo in range(0, d, PW1_COLS):
        a = _dot(hn, w1_ref[:, lo:lo + PW1_COLS]) + b1_ref[:, lo:lo + PW1_COLS]
        gate = (_dot(hn, w1_ref[:, d + lo:d + lo + PW1_COLS])
                + b1_ref[:, d + lo:d + lo + PW1_COLS])
        glu = a * jax.nn.sigmoid(gate)
        for r in range(tm // SUBLANES):
            for c in range(PW1_COLS // LANES):
                row0 = (HALO31 + r * SUBLANES) * SUBLANES + lo // LANES + c
                tok_ref[pl.ds(row0, SUBLANES, stride=SUBLANES), :] = (
                    glu[r * SUBLANES:(r + 1) * SUBLANES, c * LANES:(c + 1) * LANES])

    first = HALO31 - (kw - 1)

    def conv_block(t0):
        acc = [None] * CONV_TOKENS
        for j in range(CONV_TOKENS + kw - 1):
            row = (t0 + first + j) * SUBLANES
            v = tok_ref[row:row + SUBLANES, :]
            for t in range(max(0, j - kw + 1), min(CONV_TOKENS, j + 1)):
                term = v * wdw_ref[j - t]
                acc[t] = term if acc[t] is None else acc[t] + term
        for t in range(CONV_TOKENS):
            row = (t0 + t) * SUBLANES
            cout_ref[row:row + SUBLANES, :] = acc[t] + bdw_ref[...]

    def norm_block(t0, late_zero):
        parts = []
        for s in range(LN_ROWS // SUBLANES):
            start = (t0 + s * SUBLANES) * SUBLANES
            parts.append(jnp.concatenate(
                [cout_ref[pl.ds(start + c, SUBLANES, stride=SUBLANES), :] + late_zero
                 for c in range(nchunk)], axis=-1))
        y = jnp.concatenate(parts, axis=0)
        mu = jnp.mean(y, axis=-1, keepdims=True)
        cen = y - mu
        var = jnp.mean(cen * cen, axis=-1, keepdims=True)
        v = cen * lax.rsqrt(var + LN_EPS) * lng_ref[...] + lnb_ref[...]
        act_ref[t0:t0 + LN_ROWS, :] = (v * jax.nn.sigmoid(v)).astype(jnp.bfloat16)

    work = [functools.partial(conv_block, t0) for t0 in range(0, tm, CONV_TOKENS)]
    per_dot = -(-len(work) // (1 + 2 * len(FFN_CHUNKS_L1)))

    def emit_convs(n=per_dot):
        for _ in range(min(n, len(work))):
            work.pop(0)()

    norm_blocks = list(range(0, tm, LN_ROWS))
    per_chunk = -(-len(norm_blocks) // len(FFN_CHUNKS_L1))

    def emit_norms(gu):
        emit_convs(len(work))
        late_zero = _zero_after(gu[0:BF16_ROWS, gu.shape[1] - LANES:])
        for _ in range(min(per_chunk, len(norm_blocks))):
            norm_block(norm_blocks.pop(0), late_zero)

    emit_convs()
    h = _swiglu_block(h, g2_ref, wg_ref, wu_ref, wd_ref, FFN_CHUNKS_L1, gate_up_first=True,
                      after_dot=emit_convs, after_silu=emit_norms)
    o_ref[...] = _rms_norm(h, gf_ref[...])


def _full_spec(shape):
    return pl.BlockSpec(shape, lambda i: (0,) * len(shape))


def _call(body, h, params, weights, scratch, grid, in_map, out_map):
    n, d = h.shape
    weight_scratch = [pltpu.VMEM(w.shape[1:], jnp.bfloat16) for w in weights]
    staging = [pltpu.VMEM((2, d, STAGE_COLS), jnp.float32),
               pltpu.VMEM((2, STAGE_ROWS, d), jnp.float32),
               pltpu.SemaphoreType.DMA((2,))]
    return pl.pallas_call(
        body,
        grid=(grid,),
        in_specs=([pl.BlockSpec((TILE_M, d), in_map)] + [_full_spec(p.shape) for p in params]
                  + [pl.BlockSpec(memory_space=pl.ANY) for _ in weights]),
        out_specs=pl.BlockSpec((TILE_M, d), out_map),
        out_shape=jax.ShapeDtypeStruct((n, d), jnp.float32),
        scratch_shapes=scratch + weight_scratch + staging,
        compiler_params=pltpu.CompilerParams(
            dimension_semantics=("arbitrary",), vmem_limit_bytes=VMEM_LIMIT),
    )(h, *params, *weights)


def kernel(x, a_norm, a_w_in, a_conv, a_w_out, b_norm, b_w_pw1, b_b_pw1, b_conv, b_b_conv,
           b_ln_g, b_ln_b, b_w_pw2, b_b_pw2, ffn_norm, ffn_w_gate, ffn_w_up, ffn_w_down,
           final_norm):
    batch, seq, d = x.shape
    assert seq % TILE_M == 0
    assert a_norm.shape[0] == 1 and b_norm.shape[0] == 1 and ffn_norm.shape[0] == 2
    tiles_per_seq = seq // TILE_M
    n_tiles = batch * tiles_per_seq
    row = lambda v: v.reshape(1, -1)

    h = x.reshape(batch * seq, d)
    h = _call(
        functools.partial(_layer0_kernel, tiles_per_seq), h,
        [row(a_norm[0]), a_conv[0], row(ffn_norm[0])],
        [a_w_in, a_w_out, ffn_w_gate, ffn_w_up, ffn_w_down],
        [pltpu.VMEM((TILE_M + HALO3, d), jnp.float32)],
        n_tiles, lambda i: (i, 0), lambda i: (i, 0))
    h = _call(
        functools.partial(_layer1_kernel, tiles_per_seq), h,
        [row(b_norm[0]), row(b_b_pw1[0]),
         b_conv[0].reshape(-1, SUBLANES, LANES), b_b_conv[0].reshape(SUBLANES, LANES),
         row(b_ln_g[0]), row(b_ln_b[0]), row(b_b_pw2[0]), row(ffn_norm[1]), row(final_norm)],
        [b_w_pw1, b_w_pw2, ffn_w_gate, ffn_w_up, ffn_w_down],
        [pltpu.VMEM(((TILE_M + HALO31) * SUBLANES, LANES), jnp.float32),
         pltpu.VMEM((TILE_M * SUBLANES, LANES), jnp.float32),
         pltpu.VMEM((TILE_M, d), jnp.bfloat16),
         pltpu.VMEM((TILE_M, d), jnp.float32)],
        n_tiles + 1,
        lambda i: (jnp.minimum(i, n_tiles - 1), 0),
        lambda i: (jnp.maximum(i - 1, 0), 0))
    return h.reshape(batch, seq, d)
```

```python
import functools

import jax
import jax.numpy as jnp
from jax import lax
from jax.experimental import pallas as pl
from jax.experimental.pallas import tpu as pltpu

RMS_EPS = 1e-6
LN_EPS = 1e-5

SUBLANES = 8
LANES = 128
BF16_ROWS = 16
MXU_N = 256
STAGE_COLS = 512
STAGE_ROWS = 256
PW1_COLS = 256
TILE_M = 512
HALO3 = SUBLANES
HALO31 = 4 * SUBLANES
CONV_TOKENS = 128
LN_ROWS = 32
FFN_CHUNKS_L0 = (768, 768, 768, 512)
FFN_CHUNKS_L1 = (512, 512, 512, 512, 512, 256)
VMEM_LIMIT = 60 * 1024 * 1024


def _rms_norm(x, g):
    y = x * lax.rsqrt(jnp.mean(x * x, axis=-1, keepdims=True) + RMS_EPS)
    return y * g


def _dot(a, b):
    return jnp.dot(a, b, preferred_element_type=jnp.float32)


def _swiglu_block(h, g_ref, wg_ref, wu_ref, wd_ref, chunks, gate_up_first=False,
                  after_dot=lambda: None, after_silu=lambda gu: None):
    hn = _rms_norm(h, g_ref[...]).astype(jnp.bfloat16)
    assert sum(chunks) == wg_ref.shape[1] and all(c % MXU_N == 0 for c in chunks)
    bounds = [(sum(chunks[:c]), sum(chunks[:c + 1])) for c in range(len(chunks))]

    def gate_up(lo, hi):
        gate = _dot(hn, wg_ref[:, lo:hi])
        after_dot()
        up = _dot(hn, wu_ref[:, lo:hi])
        after_dot()
        return gate, up

    def down(acc, gate, up, lo, hi):
        gu = (gate * jax.nn.sigmoid(gate) * up).astype(jnp.bfloat16)
        after_silu(gu)
        return acc + _dot(gu, wd_ref[lo:hi, :])

    acc = h
    if gate_up_first:
        pairs = [gate_up(lo, hi) for lo, hi in bounds]
        for (gate, up), (lo, hi) in zip(pairs, bounds):
            acc = down(acc, gate, up, lo, hi)
    else:
        for lo, hi in bounds:
            gate, up = gate_up(lo, hi)
            acc = down(acc, gate, up, lo, hi)
    return acc


def _zero_after(v):
    bits = pltpu.bitcast(v, jnp.uint32)
    bits = jnp.right_shift(jnp.right_shift(bits, jnp.uint32(16)), jnp.uint32(16))
    return pltpu.bitcast(bits, jnp.float32)


def _load_weights_bf16(pairs, stage_c, stage_r, sem):
    d = stage_c.shape[1]
    chunks = []
    for src, dst in pairs:
        rows, cols = dst.shape
        if rows == d:
            for lo in range(0, cols, STAGE_COLS):
                w = min(STAGE_COLS, cols - lo)
                chunks.append((src.at[:, lo:lo + w], dst.at[:, lo:lo + w], stage_c,
                               (slice(None), slice(0, w))))
        else:
            assert cols == d and rows % STAGE_ROWS == 0
            for lo in range(0, rows, STAGE_ROWS):
                chunks.append((src.at[lo:lo + STAGE_ROWS, :], dst.at[lo:lo + STAGE_ROWS, :],
                               stage_r, (slice(None), slice(None))))
    copies = [pltpu.make_async_copy(src, stage.at[(k % 2,) + idx], sem.at[k % 2])
              for k, (src, _, stage, idx) in enumerate(chunks)]
    copies[0].start()
    for k, (_, dst, stage, idx) in enumerate(chunks):
        if k + 1 < len(chunks):
            copies[k + 1].start()
        copies[k].wait()
        dst[...] = stage[(k % 2,) + idx].astype(jnp.bfloat16)


def _layer0_kernel(tiles_per_seq, x_ref, g1_ref, wconv_ref, g2_ref,
                   win_hbm, wout_hbm, wg_hbm, wu_hbm, wd_hbm, o_ref, cv_ref,
                   win_ref, wout_ref, wg_ref, wu_ref, wd_ref, stage_c, stage_r, sem):
    tm, d = x_ref.shape
    i = pl.program_id(0)

    @pl.when(i == 0)
    def _():
        _load_weights_bf16(
            [(win_hbm.at[0], win_ref), (wout_hbm.at[0], wout_ref), (wg_hbm.at[0], wg_ref),
             (wu_hbm.at[0], wu_ref), (wd_hbm.at[0], wd_ref)], stage_c, stage_r, sem)

    @pl.when(i % tiles_per_seq == 0)
    def _():
        cv_ref[0:HALO3, :] = jnp.zeros((HALO3, d), jnp.float32)

    @pl.when(i % tiles_per_seq != 0)
    def _():
        cv_ref[0:HALO3, :] = cv_ref[tm:tm + HALO3, :]

    x = x_ref[...]
    hn = _rms_norm(x, g1_ref[...]).astype(jnp.bfloat16)
    bcv = _dot(hn, win_ref[...])
    cv_ref[HALO3:HALO3 + tm, :] = bcv[:, d:2 * d] * bcv[:, 2 * d:3 * d]
    conv = (cv_ref[HALO3 - 2:HALO3 - 2 + tm, :] * wconv_ref[0:1, :]
            + cv_ref[HALO3 - 1:HALO3 - 1 + tm, :] * wconv_ref[1:2, :]
            + cv_ref[HALO3:HALO3 + tm, :] * wconv_ref[2:3, :])
    y = (bcv[:, 0:d] * conv).astype(jnp.bfloat16)
    h = x + _dot(y, wout_ref[...])
    o_ref[...] = _swiglu_block(h, g2_ref, wg_ref, wu_ref, wd_ref, FFN_CHUNKS_L0, gate_up_first=True)


def _layer1_kernel(tiles_per_seq, x_ref, g1_ref, b1_ref, wdw_ref, bdw_ref, lng_ref, lnb_ref,
                   b2_ref, g2_ref, gf_ref, w1_hbm, w2_hbm, wg_hbm, wu_hbm, wd_hbm,
                   o_ref, tok_ref, cout_ref, act_ref, xres_ref,
                   w1_ref, w2_ref, wg_ref, wu_ref, wd_ref, stage_c, stage_r, sem):
    tm, d = x_ref.shape
    kw = wdw_ref.shape[0]
    nchunk = d // LANES
    assert nchunk == SUBLANES
    i = pl.program_id(0)

    @pl.when(i == 0)
    def _():
        act_ref[...] = jnp.zeros(act_ref.shape, act_ref.dtype)
        xres_ref[...] = jnp.zeros(xres_ref.shape, xres_ref.dtype)
        _load_weights_bf16(
            [(w1_hbm.at[0], w1_ref), (w2_hbm.at[0], w2_ref), (wg_hbm.at[1], wg_ref),
             (wu_hbm.at[1], wu_ref), (wd_hbm.at[1], wd_ref)], stage_c, stage_r, sem)

    @pl.when(i % tiles_per_seq == 0)
    def _():
        tok_ref[0:HALO31 * SUBLANES, :] = jnp.zeros((HALO31 * SUBLANES, LANES), jnp.float32)

    @pl.when(i % tiles_per_seq != 0)
    def _():
        tok_ref[0:HALO31 * SUBLANES, :] = tok_ref[tm * SUBLANES:(tm + HALO31) * SUBLANES, :]

    h = xres_ref[...] + _dot(act_ref[...], w2_ref[...]) + b2_ref[...]

    x = x_ref[...]
    xres_ref[...] = x
    hn = _rms_norm(x, g1_ref[...]).astype(jnp.bfloat16)
    for lo in range(0, d, PW1_COLS):
        a = _dot(hn, w1_ref[:, lo:lo + PW1_COLS]) + b1_ref[:, lo:lo + PW1_COLS]
        gate = (_dot(hn, w1_ref[:, d + lo:d + lo + PW1_COLS])
                + b1_ref[:, d + lo:d + lo + PW1_COLS])
        glu = a * jax.nn.sigmoid(gate)
        for r in range(tm // SUBLANES):
            for c in range(PW1_COLS // LANES):
                row0 = (HALO31 + r * SUBLANES) * SUBLANES + lo // LANES + c
                tok_ref[pl.ds(row0, SUBLANES, stride=SUBLANES), :] = (
                    glu[r * SUBLANES:(r + 1) * SUBLANES, c * LANES:(c + 1) * LANES])

    first = HALO31 - (kw - 1)

    def conv_block(t0):
        acc = [None] * CONV_TOKENS
        for j in range(CONV_TOKENS + kw - 1):
            row = (t0 + first + j) * SUBLANES
            v = tok_ref[row:row + SUBLANES, :]
            for t in range(max(0, j - kw + 1), min(CONV_TOKENS, j + 1)):
                term = v * wdw_ref[j - t]
                acc[t] = term if acc[t] is None else acc[t] + term
        for t in range(CONV_TOKENS):
            row = (t0 + t) * SUBLANES
            cout_ref[row:row + SUBLANES, :] = acc[t] + bdw_ref[...]

    def norm_block(t0, late_zero):
        parts = []
        for s in range(LN_ROWS // SUBLANES):
            start = (t0 + s * SUBLANES) * SUBLANES
            cols = [cout_ref[pl.ds(start + c, SUBLANES, stride=SUBLANES), :] for c in range(nchunk)]
            cols[0] = cols[0] + late_zero
            parts.append(jnp.concatenate(cols, axis=-1))
        y = jnp.concatenate(parts, axis=0)
        mu = jnp.mean(y, axis=-1, keepdims=True)
        cen = y - mu
        var = jnp.mean(cen * cen, axis=-1, keepdims=True)
        v = cen * lax.rsqrt(var + LN_EPS) * lng_ref[...] + lnb_ref[...]
        act_ref[t0:t0 + LN_ROWS, :] = (v * jax.nn.sigmoid(v)).astype(jnp.bfloat16)

    work = [functools.partial(conv_block, t0) for t0 in range(0, tm, CONV_TOKENS)]
    per_dot = -(-len(work) // (1 + 2 * len(FFN_CHUNKS_L1)))

    def emit_convs(n=per_dot):
        for _ in range(min(n, len(work))):
            work.pop(0)()

    norm_blocks = list(range(0, tm, LN_ROWS))
    per_chunk = -(-len(norm_blocks) // len(FFN_CHUNKS_L1))

    def emit_norms(gu):
        emit_convs(len(work))
        late_zero = _zero_after(gu[0:BF16_ROWS, gu.shape[1] - LANES:])
        for _ in range(min(per_chunk, len(norm_blocks))):
            norm_block(norm_blocks.pop(0), late_zero)

    emit_convs()
    h = _swiglu_block(h, g2_ref, wg_ref, wu_ref, wd_ref, FFN_CHUNKS_L1, gate_up_first=True,
                      after_dot=emit_convs, after_silu=emit_norms)
    o_ref[...] = _rms_norm(h, gf_ref[...])


def _full_spec(shape):
    return pl.BlockSpec(shape, lambda i: (0,) * len(shape))


def _call(body, h, params, weights, scratch, grid, in_map, out_map):
    n, d = h.shape
    weight_scratch = [pltpu.VMEM(w.shape[1:], jnp.bfloat16) for w in weights]
    staging = [pltpu.VMEM((2, d, STAGE_COLS), jnp.float32),
               pltpu.VMEM((2, STAGE_ROWS, d), jnp.float32),
               pltpu.SemaphoreType.DMA((2,))]
    return pl.pallas_call(
        body,
        grid=(grid,),
        in_specs=([pl.BlockSpec((TILE_M, d), in_map)] + [_full_spec(p.shape) for p in params]
                  + [pl.BlockSpec(memory_space=pl.ANY) for _ in weights]),
        out_specs=pl.BlockSpec((TILE_M, d), out_map),
        out_shape=jax.ShapeDtypeStruct((n, d), jnp.float32),
        scratch_shapes=scratch + weight_scratch + staging,
        compiler_params=pltpu.CompilerParams(
            dimension_semantics=("arbitrary",), vmem_limit_bytes=VMEM_LIMIT),
    )(h, *params, *weights)


def kernel(x, a_norm, a_w_in, a_conv, a_w_out, b_norm, b_w_pw1, b_b_pw1, b_conv, b_b_conv,
           b_ln_g, b_ln_b, b_w_pw2, b_b_pw2, ffn_norm, ffn_w_gate, ffn_w_up, ffn_w_down,
           final_norm):
    batch, seq, d = x.shape
    assert seq % TILE_M == 0
    assert a_norm.shape[0] == 1 and b_norm.shape[0] == 1 and ffn_norm.shape[0] == 2
    tiles_per_seq = seq // TILE_M
    n_tiles = batch * tiles_per_seq
    row = lambda v: v.reshape(1, -1)

    h = x.reshape(batch * seq, d)
    h = _call(
        functools.partial(_layer0_kernel, tiles_per_seq), h,
        [row(a_norm[0]), a_conv[0], row(ffn_norm[0])],
        [a_w_in, a_w_out, ffn_w_gate, ffn_w_up, ffn_w_down],
        [pltpu.VMEM((TILE_M + HALO3, d), jnp.float32)],
        n_tiles, lambda i: (i, 0), lambda i: (i, 0))
    h = _call(
        functools.partial(_layer1_kernel, tiles_per_seq), h,
        [row(b_norm[0]), row(b_b_pw1[0]),
         b_conv[0].reshape(-1, SUBLANES, LANES), b_b_conv[0].reshape(SUBLANES, LANES),
         row(b_ln_g[0]), row(b_ln_b[0]), row(b_b_pw2[0]), row(ffn_norm[1]), row(final_norm)],
        [b_w_pw1, b_w_pw2, ffn_w_gate, ffn_w_up, ffn_w_down],
        [pltpu.VMEM(((TILE_M + HALO31) * SUBLANES, LANES), jnp.float32),
         pltpu.VMEM((TILE_M * SUBLANES, LANES), jnp.float32),
         pltpu.VMEM((TILE_M, d), jnp.bfloat16),
         pltpu.VMEM((TILE_M, d), jnp.float32)],
        n_tiles + 1,
        lambda i: (jnp.minimum(i, n_tiles - 1), 0),
        lambda i: (jnp.maximum(i - 1, 0), 0))
    return h.reshape(batch, seq, d)
```
